```python
import math
import jax
import jax.numpy as jnp
from jax import lax
import numpy as np

D_MODEL = 1024
BATCH = 32
SEQ = 256
DEPTH = 2
DEC_BATCH = 8
DEC_SEQ = 4096
PAST_LEN = 256

GRID_W = 64
N_EVEN = (DEPTH + 1) // 2
N_ODD = DEPTH // 2
D_FOURIER = D_MODEL // 2
FOURIER_GROUPS = 4
FOURIER_GROUP_DIM = D_FOURIER // FOURIER_GROUPS
D_POOL = D_MODEL - D_FOURIER
POOL_WINDOWS = (2, 4, 8, 16)
POOL_GROUPS = len(POOL_WINDOWS)
POOL_GROUP_DIM = D_POOL // POOL_GROUPS
N_HEADS = 8
HEAD_DIM = D_MODEL // (2 * N_HEADS)
VALUE_DIM = 2 * HEAD_DIM
ROPE_THETA = 10000.0
Q_BLOCK = 128
N_EXPERTS = 16
N_EXPERT_GROUPS = 4
EXPERTS_PER_GROUP = N_EXPERTS // N_EXPERT_GROUPS
TOP_K = 2
D_EXPERT = D_MODEL // 2
NORM_EPS = 1e-6
SUBLN_EPS = 1e-5

kernel_name = "hybrid_fourier_pool_diffattn_moe_diffusion_step"

F32 = jnp.float32


def rms_norm(x, g, eps=NORM_EPS):
    xf = x.astype(F32)
    y = xf * lax.rsqrt(jnp.mean(xf * xf, axis=-1, keepdims=True) + eps)
    return (y * g.astype(F32)).astype(x.dtype)


def adaln_params(cond, w, b):
    m = jax.nn.silu(cond) @ w + b
    return jnp.split(m, 6, axis=-1)


def modulate(x, g, shift, scale):
    return rms_norm(x, g) * (1 + scale) + shift


def fourier_mixer(u, w_fourier):
    b, n, _ = u.shape
    ug = u.reshape(b, n, FOURIER_GROUPS, FOURIER_GROUP_DIM).astype(F32)
    f = jnp.fft.fft2(ug, axes=(1, 3), norm="ortho").real
    out = jnp.einsum('bngc,gcd->bngd', f.astype(u.dtype), w_fourier)
    return out.reshape(b, n, D_FOURIER)


def pool_mixer(u, w_pool, pool_scale):
    b, n, _ = u.shape
    ug = u.reshape(b, n, POOL_GROUPS, POOL_GROUP_DIM).astype(F32)
    csum = jnp.concatenate([jnp.zeros_like(ug[:, :1]), jnp.cumsum(ug, axis=1)], axis=1)
    t = np.arange(n)[:, None]
    half = np.array(POOL_WINDOWS)[None, :] // 2
    lo = np.clip(t - half, 0, n)
    hi = np.clip(t + half, 0, n)
    gidx = np.arange(POOL_GROUPS)
    window_sum = csum[:, hi, gidx] - csum[:, lo, gidx]
    count = (hi - lo).astype(np.float32)[None, :, :, None]
    mixed = (window_sum / count - ug).astype(u.dtype)
    out = jnp.einsum('bngc,gcd->bngd', mixed, w_pool).reshape(b, n, D_POOL)
    return out * pool_scale


def axial_rope_tables(n):
    rows = n // GRID_W
    r, c = jnp.meshgrid(jnp.arange(rows, dtype=F32), jnp.arange(GRID_W, dtype=F32), indexing='ij')
    r = r.reshape(-1)
    c = c.reshape(-1)
    quarter = HEAD_DIM // 4
    inv_freq = 1.0 / (ROPE_THETA ** (jnp.arange(quarter, dtype=F32) / quarter))
    ang = jnp.stack([r[:, None] * inv_freq, c[:, None] * inv_freq], axis=1)
    ang = jnp.stack([ang, ang], axis=2)
    return jnp.cos(ang).reshape(n, HEAD_DIM), jnp.sin(ang).reshape(n, HEAD_DIM)


def apply_axial_rope(x, cos, sin):
    xs = x.reshape(x.shape[:-1] + (2, 2, HEAD_DIM // 4))
    rot = jnp.stack([-xs[..., 1, :], xs[..., 0, :]], axis=-2).reshape(x.shape)
    cos = cos[:, None, None, :]
    sin = sin[:, None, None, :]
    return (x.astype(F32) * cos + rot.astype(F32) * sin).astype(x.dtype)


def diff_qkv(h, w_qkv):
    b, n, _ = h.shape
    q, k, v = jnp.split(h @ w_qkv, 3, axis=-1)
    q = q.reshape(b, n, N_HEADS, 2, HEAD_DIM)
    k = k.reshape(b, n, N_HEADS, 2, HEAD_DIM)
    v = v.reshape(b, n, N_HEADS, VALUE_DIM)
    return q, k, v


def diff_lambda(lq1, lk1, lq2, lk2, lam_init):
    return (jnp.exp(jnp.sum(lq1.astype(F32) * lk1.astype(F32)))
            - jnp.exp(jnp.sum(lq2.astype(F32) * lk2.astype(F32))) + lam_init)


def diff_attention(q, k, v, lam):
    b, lq = q.shape[:2]
    nb = lq // Q_BLOCK
    scale = HEAD_DIM ** -0.5
    vf = v.astype(F32)

    def one_block(qb):
        s = jnp.einsum('bqhid,bkhid->bhiqk', qb, k, preferred_element_type=F32) * scale
        p = jax.nn.softmax(s, axis=-1)
        a = p[:, :, 0] - lam * p[:, :, 1]
        return jnp.einsum('bhqk,bkhe->bqhe', a, vf)

    qb = q.reshape(b, nb, Q_BLOCK, N_HEADS, 2, HEAD_DIM).swapaxes(0, 1)
    out = lax.map(one_block, qb)
    return out.swapaxes(0, 1).reshape(b, lq, N_HEADS, VALUE_DIM)


def diff_output(o, subln_g, w_o, lam_init, dtype):
    b, n = o.shape[:2]
    o = rms_norm(o, subln_g, SUBLN_EPS) * (1.0 - lam_init)
    return o.astype(dtype).reshape(b, n, N_HEADS * VALUE_DIM) @ w_o


def grouped_moe(h, w_router, router_bias, w_gate, w_up, w_down):
    b, n, dm = h.shape
    t = h.reshape(b * n, dm)
    s = jax.nn.sigmoid((t @ w_router).astype(F32))
    sel = s + router_bias.astype(F32)
    group_score = lax.top_k(sel.reshape(-1, N_EXPERT_GROUPS, EXPERTS_PER_GROUP), 2)[0].sum(-1)
    best_group = jnp.argmax(group_score, axis=-1)
    in_group = (jnp.arange(N_EXPERTS) // EXPERTS_PER_GROUP)[None, :] == best_group[:, None]
    _, eidx = lax.top_k(jnp.where(in_group, sel, -jnp.inf), TOP_K)
    w = jnp.take_along_axis(s, eidx, axis=-1)
    w = w / jnp.sum(w, axis=-1, keepdims=True)
    gate = jnp.sum(jax.nn.one_hot(eidx, N_EXPERTS, dtype=F32) * w[..., None], axis=1)
    y = jnp.zeros((b * n, dm), F32)
    for e in range(N_EXPERTS):
        he = jax.nn.silu(t @ w_gate[e]) * (t @ w_up[e])
        y = y + gate[:, e:e + 1] * (he @ w_down[e]).astype(F32)
    return y.astype(h.dtype).reshape(b, n, dm)


def setup_inputs(seed: int = 0) -> dict:
    key = jax.random.key(seed)
    ks = iter(jax.random.split(key, 32))

    def nrm(shape, s):
        return jax.random.normal(next(ks), shape, F32) * s

    d = D_MODEL
    return {
        "x_prompt": nrm((BATCH, SEQ, d), 1.0),
        "x_sample": nrm((DEC_BATCH, DEC_SEQ, d), 1.0),
        "cache_k": nrm((DEC_BATCH, N_ODD, PAST_LEN, N_HEADS, 2 * HEAD_DIM), 1.0),
        "cache_v": nrm((DEC_BATCH, N_ODD, PAST_LEN, N_HEADS, VALUE_DIM), 1.0),
        "c": nrm((DEC_BATCH, d), 1.0),
        "c_ctx": nrm((d,), 1.0),
        "norm1_g": 1.0 + nrm((DEPTH, d), 0.01),
        "norm2_g": 1.0 + nrm((DEPTH, d), 0.01),
        "final_norm_g": 1.0 + nrm((d,), 0.01),
        "w_ada": nrm((DEPTH, d, 6 * d), 0.5 * d ** -0.5),
        "b_ada": nrm((DEPTH, 6 * d), 0.02),
        "w_in_ab": nrm((N_EVEN, d, D_FOURIER + D_POOL), d ** -0.5),
        "w_fourier": nrm((N_EVEN, FOURIER_GROUPS, FOURIER_GROUP_DIM, FOURIER_GROUP_DIM), FOURIER_GROUP_DIM ** -0.5),
        "w_pool": nrm((N_EVEN, POOL_GROUPS, POOL_GROUP_DIM, POOL_GROUP_DIM), POOL_GROUP_DIM ** -0.5),
        "pool_scale": 0.5 + nrm((N_EVEN, D_POOL), 0.1),
        "w_out_ab": nrm((N_EVEN, D_FOURIER + D_POOL, d), d ** -0.5),
        "w_qkv": nrm((N_ODD, d, 3 * d), d ** -0.5),
        "lambda_q1": nrm((N_ODD, HEAD_DIM), 0.1),
        "lambda_k1": nrm((N_ODD, HEAD_DIM), 0.1),
        "lambda_q2": nrm((N_ODD, HEAD_DIM), 0.1),
        "lambda_k2": nrm((N_ODD, HEAD_DIM), 0.1),
        "subln_g": 1.0 + nrm((N_ODD, VALUE_DIM), 0.01),
        "w_o": nrm((N_ODD, N_HEADS * VALUE_DIM, d), d ** -0.5),
        "w_router": nrm((d, N_EXPERTS), d ** -0.5),
        "router_bias": nrm((N_EXPERTS,), 0.01),
        "w_gate": nrm((DEPTH, N_EXPERTS, d, D_EXPERT), d ** -0.5),
        "w_up": nrm((DEPTH, N_EXPERTS, d, D_EXPERT), d ** -0.5),
        "w_down": nrm((DEPTH, N_EXPERTS, D_EXPERT, d), D_EXPERT ** -0.5),
    }


def reference(x_prompt, x_sample, cache_k, cache_v, c, c_ctx, norm1_g, norm2_g, final_norm_g,
              w_ada, b_ada, w_in_ab, w_fourier, w_pool, pool_scale, w_out_ab, w_qkv,
              lambda_q1, lambda_k1, lambda_q2, lambda_k2, subln_g, w_o, w_router, router_bias,
              w_gate, w_up, w_down):

    def mixer_ab(h, j):
        u = h @ w_in_ab[j]
        ya = fourier_mixer(u[..., :D_FOURIER], w_fourier[j])
        yb = pool_mixer(u[..., D_FOURIER:], w_pool[j], pool_scale[j])
        return jnp.concatenate([ya, yb], axis=-1) @ w_out_ab[j]

    def layer_lambda(l, j):
        lam_init = 0.8 - 0.6 * math.exp(-0.3 * l)
        lam = diff_lambda(lambda_q1[j], lambda_k1[j], lambda_q2[j], lambda_k2[j], lam_init)
        return lam_init, lam

    def channel_mix(x, l, shift, scale, gate):
        h = modulate(x, norm2_g[l], shift, scale)
        return x + gate * grouped_moe(h, w_router, router_bias, w_gate[l], w_up[l], w_down[l])

    xp = x_prompt
    ctx_k, ctx_v = [], []
    for l in range(DEPTH):
        j = l // 2
        sh1, sc1, g1, sh2, sc2, g2 = adaln_params(c_ctx[None, None, :], w_ada[l], b_ada[l])
        h = modulate(xp, norm1_g[l], sh1, sc1)
        if l % 2 == 0:
            y = mixer_ab(h, j)
        else:
            lam_init, lam = layer_lambda(l, j)
            q, k, v = diff_qkv(h, w_qkv[j])
            y = diff_output(diff_attention(q, k, v, lam), subln_g[j], w_o[j], lam_init, xp.dtype)
            ctx_k.append(k.reshape(k.shape[:3] + (2 * HEAD_DIM,)))
            ctx_v.append(v)
        xp = xp + g1 * y
        xp = channel_mix(xp, l, sh2, sc2, g2)
    y_prompt = rms_norm(xp, final_norm_g)
    new_cache_k = jnp.stack(ctx_k, axis=1)
    new_cache_v = jnp.stack(ctx_v, axis=1)

    xs = x_sample
    b, n, _ = xs.shape
    cos, sin = axial_rope_tables(n)
    for l in range(DEPTH):
        j = l // 2
        sh1, sc1, g1, sh2, sc2, g2 = adaln_params(c[:, None, :], w_ada[l], b_ada[l])
        h = modulate(xs, norm1_g[l], sh1, sc1)
        if l % 2 == 0:
            y = mixer_ab(h, j)
        else:
            lam_init, lam = layer_lambda(l, j)
            q, k, v = diff_qkv(h, w_qkv[j])
            q = apply_axial_rope(q, cos, sin)
            k = apply_axial_rope(k, cos, sin)
            kc = cache_k[:, j]
            kc = kc.reshape(kc.shape[:3] + (2, HEAD_DIM))
            k_all = jnp.concatenate([kc.astype(k.dtype), k], axis=1)
            v_all = jnp.concatenate([cache_v[:, j].astype(v.dtype), v], axis=1)
            y = diff_output(diff_attention(q, k_all, v_all, lam), subln_g[j], w_o[j], lam_init, xs.dtype)
        xs = xs + g1 * y
        xs = channel_mix(xs, l, sh2, sc2, g2)
    y_sample = rms_norm(xs, final_norm_g)

    return (y_prompt, y_sample, new_cache_k, new_cache_v)
```

```python
import functools
import math

import jax
import jax.numpy as jnp
from jax import lax
from jax.experimental import pallas as pl
from jax.experimental.pallas import tpu as pltpu

F32 = jnp.float32
BF16 = jnp.bfloat16
I32 = jnp.int32

D_MODEL = 1024
GRID_W = 64
D_FOURIER = 512
GROUP_DIM = 128
N_GROUPS = 4
POOL_WINDOWS = (2, 4, 8, 16)
N_HEADS = 8
HEAD_DIM = 64
VALUE_DIM = 128
ROPE_THETA = 10000.0
N_EXPERTS = 16
EXPERTS_PER_GROUP = 4
PAIRS_PER_GROUP = 6
N_CLASSES = 24
D_EXPERT = 512
NORM_EPS = 1e-6
SUBLN_EPS = 1e-5

V7X_VMEM_BYTES = 64 * 1024 * 1024
VMEM_LIMIT = V7X_VMEM_BYTES * 3 // 4
LANES = 128
MOE_TILE = 256
POOL_TILE = 256
ROUTE_ROWS = 8

_NT = (((1,), (1,)), ((), ()))


def _cparams(*sem):
    return pltpu.CompilerParams(dimension_semantics=sem, vmem_limit_bytes=VMEM_LIMIT)


def _dot(a, b):
    return jnp.dot(a, b, preferred_element_type=F32)


def _split_bf16(x):
    hi = x.astype(BF16)
    lo = (x - hi.astype(F32)).astype(BF16)
    return hi, lo


def _modulate(x, g, shift, scale):
    ms = jnp.mean(x * x, axis=-1, keepdims=True)
    y = x * lax.rsqrt(ms + NORM_EPS)
    return (y * g) * (1.0 + scale) + shift


def _mod_part(m, k):
    return m[:, k * D_MODEL:(k + 1) * D_MODEL]


def _ada_kernel(c_ref, w_ref, b_ref, o_ref):
    c = c_ref[...]
    hi, lo = _split_bf16(c * jax.nn.sigmoid(c))
    whi, wlo = _split_bf16(w_ref[0])
    o_ref[0] = _dot(hi, whi) + _dot(lo, whi) + _dot(hi, wlo) + b_ref[0]


def _ada_params(cond, w_ada, b_ada):
    n_layers, d, d6 = w_ada.shape
    r = cond.shape[0]
    tn = d6 // 4
    return pl.pallas_call(
        _ada_kernel,
        out_shape=jax.ShapeDtypeStruct((n_layers, r, d6), F32),
        grid=(n_layers, d6 // tn),
        in_specs=[pl.BlockSpec((r, d), lambda l, j: (0, 0)),
                  pl.BlockSpec((1, d, tn), lambda l, j: (l, 0, j)),
                  pl.BlockSpec((1, 1, tn), lambda l, j: (l, 0, j))],
        out_specs=pl.BlockSpec((1, r, tn), lambda l, j: (l, 0, j)),
        compiler_params=_cparams("arbitrary", "arbitrary"),
        name="ada_params",
    )(cond, w_ada, b_ada.reshape(n_layers, 1, d6))


def _inproj_kernel(x_ref, mod_ref, g_ref, w_ref, cs_ref, ab_ref, up_ref):
    m = mod_ref[0]
    h = _modulate(x_ref[0], g_ref[...], _mod_part(m, 0), _mod_part(m, 1)).astype(BF16)
    u = _dot(h, w_ref[...])
    for g in range(N_GROUPS):
        sl = slice(g * GROUP_DIM, (g + 1) * GROUP_DIM)
        cs = _dot(u[:, sl].astype(BF16), cs_ref[...])
        ab_ref[0, :, sl] = cs[:, :GROUP_DIM].astype(BF16)
        ab_ref[1, :, sl] = cs[:, GROUP_DIM:].astype(BF16)
    up_ref[0] = u[:, D_FOURIER:].astype(BF16)


def _inproj(x, mod, g, w_in, cs_tab, tm):
    b, n, d = x.shape
    per_batch = mod.shape[0] > 1
    return pl.pallas_call(
        _inproj_kernel,
        out_shape=(jax.ShapeDtypeStruct((2, n, b * D_FOURIER), BF16),
                   jax.ShapeDtypeStruct((b, n, d - D_FOURIER), BF16)),
        grid=(b, n // tm),
        in_specs=[pl.BlockSpec((1, tm, d), lambda bi, i: (bi, i, 0)),
                  pl.BlockSpec((1, 1, 6 * d), lambda bi, i: (bi if per_batch else 0, 0, 0)),
                  pl.BlockSpec((1, d), lambda bi, i: (0, 0)),
                  pl.BlockSpec((d, d), lambda bi, i: (0, 0)),
                  pl.BlockSpec((GROUP_DIM, 2 * GROUP_DIM), lambda bi, i: (0, 0))],
        out_specs=(pl.BlockSpec((2, tm, D_FOURIER), lambda bi, i: (0, i, bi)),
                   pl.BlockSpec((1, tm, d - D_FOURIER), lambda bi, i: (bi, i, 0))),
        compiler_params=_cparams("arbitrary", "arbitrary"),
        name="inproj",
    )(x, mod, g, w_in, cs_tab)


def _posdft_kernel(g_ref, ab_ref, wf_ref, o_ref, acc_ref, *, nk, tn):
    k = pl.program_id(2)

    @pl.when(k == 0)
    def _():
        acc_ref[...] = jnp.zeros_like(acc_ref)

    acc_ref[...] += _dot(g_ref[...], ab_ref[...])

    @pl.when(k == nk - 1)
    def _():
        for j in range(tn // GROUP_DIM):
            sl = slice(j * GROUP_DIM, (j + 1) * GROUP_DIM)
            f = acc_ref[:, sl].astype(BF16)
            o_ref[:, sl] = _dot(f, wf_ref[j % N_GROUPS]).astype(BF16)


def _posdft(gtab, ab, w_fourier, tm, tn, tk):
    n, k2 = gtab.shape
    cols = ab.shape[1]
    nk = k2 // tk
    return pl.pallas_call(
        functools.partial(_posdft_kernel, nk=nk, tn=tn),
        out_shape=jax.ShapeDtypeStruct((n, cols), BF16),
        grid=(n // tm, cols // tn, nk),
        in_specs=[pl.BlockSpec((tm, tk), lambda i, j, k: (i, k)),
                  pl.BlockSpec((tk, tn), lambda i, j, k: (k, j)),
                  pl.BlockSpec((N_GROUPS, GROUP_DIM, GROUP_DIM), lambda i, j, k: (0, 0, 0))],
        out_specs=pl.BlockSpec((tm, tn), lambda i, j, k: (i, j)),
        scratch_shapes=[pltpu.VMEM((tm, tn), F32)],
        compiler_params=_cparams("arbitrary", "arbitrary", "arbitrary"),
        name="posdft",
    )(gtab, ab, w_fourier)


def _pool_kernel(up_ref, w_ref, wp_ref, ps_ref, o_ref, *, n, win):
    j = pl.program_id(1)
    start = pl.multiple_of(jnp.clip(j * POOL_TILE - (win - POOL_TILE) // 2, 0, n - win), LANES)
    for g in range(N_GROUPS):
        sl = slice(g * GROUP_DIM, (g + 1) * GROUP_DIM)
        xg = up_ref[0, pl.ds(start, win), sl]
        mixed = _dot(w_ref[0, g], xg).astype(BF16)
        o_ref[0, :, sl] = (_dot(mixed, wp_ref[g]) * ps_ref[:, sl]).astype(BF16)


def _pool(up, wtab, w_pool, pool_scale):
    b, n, dp = up.shape
    win = wtab.shape[-1]
    n_t = n // POOL_TILE

    def widx(bi, j):
        return (jnp.where(j == 0, 0, jnp.where(j == n_t - 1, 2, 1)), 0, 0, 0)

    return pl.pallas_call(
        functools.partial(_pool_kernel, n=n, win=win),
        out_shape=jax.ShapeDtypeStruct((b, n, dp), BF16),
        grid=(b, n_t),
        in_specs=[pl.BlockSpec((1, n, dp), lambda bi, j: (bi, 0, 0)),
                  pl.BlockSpec((1, N_GROUPS, POOL_TILE, win), widx),
                  pl.BlockSpec((N_GROUPS, GROUP_DIM, GROUP_DIM), lambda bi, j: (0, 0, 0)),
                  pl.BlockSpec((1, dp), lambda bi, j: (0, 0))],
        out_specs=pl.BlockSpec((1, POOL_TILE, dp), lambda bi, j: (bi, j, 0)),
        compiler_params=_cparams("arbitrary", "arbitrary"),
        name="pool",
    )(up, wtab, w_pool, pool_scale)


def _route(h2, wr_ref, rb_ref, route_ref):
    hi, lo = _split_bf16(h2)
    whi, wlo = _split_bf16(wr_ref[...])
    lg = (lax.dot_general(whi, hi, _NT, preferred_element_type=F32)
          + lax.dot_general(whi, lo, _NT, preferred_element_type=F32)
          + lax.dot_general(wlo, hi, _NT, preferred_element_type=F32))
    s = jax.nn.sigmoid(lg)
    sel = s + rb_ref[...]
    srow = [s[e:e + 1, :] for e in range(N_EXPERTS)]
    vrow = [sel[e:e + 1, :] for e in range(N_EXPERTS)]

    best_score = None
    best_group = None
    for g in range(N_EXPERTS // EXPERTS_PER_GROUP):
        a, b, c, d = vrow[4 * g:4 * g + 4]
        hi1, lo1, hi2, lo2 = jnp.maximum(a, b), jnp.minimum(a, b), jnp.maximum(c, d), jnp.minimum(c, d)
        score = jnp.maximum(hi1, hi2) + jnp.maximum(jnp.minimum(hi1, hi2), jnp.maximum(lo1, lo2))
        if g == 0:
            best_score, best_group = score, jnp.zeros_like(score)
        else:
            better = score > best_score
            best_group = jnp.where(better, float(g), best_group)
            best_score = jnp.where(better, score, best_score)

    def pick(rows, j):
        out = rows[j]
        for g in range(1, N_EXPERTS // EXPERTS_PER_GROUP):
            out = jnp.where(best_group == float(g), rows[4 * g + j], out)
        return out

    v = [pick(vrow, j) for j in range(EXPERTS_PER_GROUP)]
    u = [pick(srow, j) for j in range(EXPERTS_PER_GROUP)]
    i0, m0 = jnp.zeros_like(v[0]), v[0]
    for j in range(1, EXPERTS_PER_GROUP):
        better = v[j] > m0
        i0 = jnp.where(better, float(j), i0)
        m0 = jnp.where(better, v[j], m0)
    i1, m1 = jnp.full_like(v[0], -1.0), jnp.full_like(v[0], -jnp.inf)
    for j in range(EXPERTS_PER_GROUP):
        better = (i0 != float(j)) & ((v[j] > m1) | (i1 < 0.0))
        i1 = jnp.where(better, float(j), i1)
        m1 = jnp.where(better, v[j], m1)

    def at(rows, idx):
        out = rows[0]
        for j in range(1, EXPERTS_PER_GROUP):
            out = jnp.where(idx == float(j), rows[j], out)
        return out

    lo_i, hi_i = jnp.minimum(i0, i1), jnp.maximum(i0, i1)
    w_lo, w_hi = at(u, lo_i), at(u, hi_i)
    denom = at(u, i0) + at(u, i1)
    pair = jnp.where(lo_i == 0.0, hi_i - 1.0, jnp.where(lo_i == 1.0, hi_i + 1.0, 5.0))
    route_ref[0:1, :] = best_group * float(PAIRS_PER_GROUP) + pair
    route_ref[1:2, :] = w_lo / denom
    route_ref[2:3, :] = w_hi / denom
    route_ref[3:ROUTE_ROWS, :] = jnp.zeros((ROUTE_ROWS - 3, route_ref.shape[1]), F32)


def _proj_kernel(*refs, n_lhs, splits):
    lhs = refs[:n_lhs]
    w_ref, x_ref, mod_ref, g_ref, wr_ref, rb_ref, x1_ref, h2_ref, route_ref = refs[n_lhs:]
    y = None
    for r, (lo, hi) in zip(lhs, splits):
        a = r[0] if len(r.shape) == 3 else r[...]
        t = _dot(a, w_ref[lo:hi, :])
        y = t if y is None else y + t
    m = mod_ref[0]
    x1 = x_ref[0] + _mod_part(m, 2) * y
    x1_ref[0] = x1
    h2 = _modulate(x1, g_ref[...], _mod_part(m, 3), _mod_part(m, 4))
    h2_ref[...] = h2
    _route(h2, wr_ref, rb_ref, route_ref)


def _proj(lhs, lhs_specs, splits, w, x, mod, g2, wr_t, rb, tm):
    b, n, d = x.shape
    nt = n // tm
    per_batch = mod.shape[0] > 1
    n_lhs = len(lhs)
    return pl.pallas_call(
        functools.partial(_proj_kernel, n_lhs=n_lhs, splits=splits),
        out_shape=(jax.ShapeDtypeStruct((b, n, d), F32),
                   jax.ShapeDtypeStruct((b * n, d), F32),
                   jax.ShapeDtypeStruct((ROUTE_ROWS, b * n), F32)),
        grid=(b, nt),
        in_specs=list(lhs_specs) + [
            pl.BlockSpec((d, d), lambda bi, i: (0, 0)),
            pl.BlockSpec((1, tm, d), lambda bi, i: (bi, i, 0)),
            pl.BlockSpec((1, 1, 6 * d), lambda bi, i: (bi if per_batch else 0, 0, 0)),
            pl.BlockSpec((1, d), lambda bi, i: (0, 0)),
            pl.BlockSpec((N_EXPERTS, d), lambda bi, i: (0, 0)),
            pl.BlockSpec((N_EXPERTS, 1), lambda bi, i: (0, 0))],
        out_specs=(pl.BlockSpec((1, tm, d), lambda bi, i: (bi, i, 0)),
                   pl.BlockSpec((tm, d), lambda bi, i: (bi * nt + i, 0)),
                   pl.BlockSpec((ROUTE_ROWS, tm), lambda bi, i: (0, bi * nt + i))),
        compiler_params=_cparams("arbitrary", "arbitrary"),
        name="proj_route",
    )(*lhs, w, x, mod, g2, wr_t, rb)


def _moe_kernel(ea_ref, eb_ref, nv_ref, gidx_hbm, sidx_hbm, h_hbm, gate_ref, wgu_a, wgu_b, wd_a, wd_b,
                y_hbm, gi_smem, si_smem, xbuf, ybuf, sem_idx, sem_g, sem_s):
    del ea_ref, eb_ref
    i = pl.program_id(0)
    n_valid = nv_ref[0]
    slot = i % 2
    tm = xbuf.shape[1]

    def fetch_idx(t, sl):
        cg = pltpu.make_async_copy(gidx_hbm.at[pl.ds(t * tm, tm)], gi_smem.at[pl.ds(sl * tm, tm)], sem_idx.at[0])
        cs = pltpu.make_async_copy(sidx_hbm.at[pl.ds(t * tm, tm)], si_smem.at[pl.ds(sl * tm, tm)], sem_idx.at[1])
        cg.start()
        cs.start()
        cg.wait()
        cs.wait()

    def gather_rows(sl):
        def body(r, carry):
            tok = gi_smem[sl * tm + r]
            pltpu.make_async_copy(h_hbm.at[pl.ds(tok, 1)], xbuf.at[sl, pl.ds(r, 1)], sem_g.at[sl]).start()
            return carry
        lax.fori_loop(0, tm, body, 0, unroll=8)

    def scatter_rows(sl):
        def body(r, carry):
            dst = si_smem[sl * tm + r]
            pltpu.make_async_copy(ybuf.at[sl, pl.ds(r, 1)], y_hbm.at[pl.ds(dst, 1)], sem_s.at[sl]).start()
            return carry
        lax.fori_loop(0, tm, body, 0, unroll=8)

    def wait_gather(sl):
        pltpu.make_async_copy(h_hbm.at[pl.ds(0, tm)], xbuf.at[sl], sem_g.at[sl]).wait()

    def wait_scatter(sl):
        pltpu.make_async_copy(ybuf.at[sl], y_hbm.at[pl.ds(0, tm)], sem_s.at[sl]).wait()

    @pl.when(i == 0)
    def _():
        fetch_idx(0, 0)
        gather_rows(0)
        n_tok = y_hbm.shape[0] - 2 * tm
        ybuf[...] = jnp.zeros_like(ybuf)
        for sl in range(2):
            pltpu.make_async_copy(ybuf.at[sl], y_hbm.at[pl.ds(n_tok + sl * tm, tm)], sem_s.at[sl]).start()
        for sl in range(2):
            wait_scatter(sl)

    @pl.when(i + 1 < n_valid)
    def _():
        fetch_idx(i + 1, 1 - slot)
        gather_rows(1 - slot)

    @pl.when(i < n_valid)
    def _():
        wait_gather(slot)

        @pl.when(i >= 2)
        def _():
            wait_scatter(slot)

        x = xbuf[slot].astype(BF16)

        def ffn(wgu, wd):
            gu = _dot(x, wgu[0])
            g, u = gu[:, :D_EXPERT], gu[:, D_EXPERT:]
            he = (g * jax.nn.sigmoid(g)) * u
            return _dot(he.astype(BF16), wd[0])

        gate = gate_ref[...]
        ybuf[slot] = gate[:, 0:1] * ffn(wgu_a, wd_a) + gate[:, 1:2] * ffn(wgu_b, wd_b)
        scatter_rows(slot)

        @pl.when(i == n_valid - 1)
        def _():
            wait_scatter(slot)

            @pl.when(i >= 1)
            def _():
                wait_scatter(1 - slot)


def _moe(h2, route, wgu, wd):
    t, d = h2.shape
    tm = MOE_TILE
    n_tiles = t // tm + N_CLASSES

    cls = route[0].astype(I32)
    counts = jnp.sum((cls[:, None] == jnp.arange(N_CLASSES, dtype=I32)[None, :]).astype(I32), axis=0)
    order = jnp.argsort(cls, stable=True).astype(I32)
    padded = ((counts + tm - 1) // tm) * tm
    pend = jnp.cumsum(padded)
    pstart = pend - padded
    ustart = jnp.cumsum(counts) - counts
    n_valid = (pend[-1] // tm).astype(I32)
    tile_ids = jnp.arange(n_tiles, dtype=I32)
    tile_cls = jnp.minimum(jnp.searchsorted(pend, tile_ids * tm, side="right").astype(I32), N_CLASSES - 1)
    tile_ok = tile_ids < n_valid
    tile_cls = jnp.where(tile_ok, tile_cls, tile_cls[n_valid - 1])
    pair_lo = jnp.array([0, 0, 0, 1, 1, 2], I32)
    pair_hi = jnp.array([1, 2, 3, 2, 3, 3], I32)
    tile_ea = (tile_cls // PAIRS_PER_GROUP) * EXPERTS_PER_GROUP + pair_lo[tile_cls % PAIRS_PER_GROUP]
    tile_eb = (tile_cls // PAIRS_PER_GROUP) * EXPERTS_PER_GROUP + pair_hi[tile_cls % PAIRS_PER_GROUP]
    slot = jnp.arange(n_tiles * tm, dtype=I32)
    st = slot // tm
    sc = tile_cls[st]
    rank = slot - pstart[sc]
    ok = tile_ok[st] & (rank < counts[sc])
    tok = order[jnp.clip(ustart[sc] + rank, 0, t - 1)]
    gidx = jnp.where(ok, tok, 0)
    sidx = jnp.where(ok, tok, t + (st % 2) * tm + slot % tm)
    gates = jnp.stack([jnp.where(ok, route[1][tok], 0.0), jnp.where(ok, route[2][tok], 0.0)], axis=1)

    grid_spec = pltpu.PrefetchScalarGridSpec(
        num_scalar_prefetch=3,
        grid=(n_tiles,),
        in_specs=[pl.BlockSpec(memory_space=pl.ANY),
                  pl.BlockSpec(memory_space=pl.ANY),
                  pl.BlockSpec(memory_space=pl.ANY),
                  pl.BlockSpec((tm, 2), lambda i, ea, eb, nv: (i, 0)),
                  pl.BlockSpec((1, d, 2 * D_EXPERT), lambda i, ea, eb, nv: (ea[i], 0, 0)),
                  pl.BlockSpec((1, d, 2 * D_EXPERT), lambda i, ea, eb, nv: (eb[i], 0, 0)),
                  pl.BlockSpec((1, D_EXPERT, d), lambda i, ea, eb, nv: (ea[i], 0, 0)),
                  pl.BlockSpec((1, D_EXPERT, d), lambda i, ea, eb, nv: (eb[i], 0, 0))],
        out_specs=pl.BlockSpec(memory_space=pl.ANY),
        scratch_shapes=[pltpu.SMEM((2 * tm,), I32),
                        pltpu.SMEM((2 * tm,), I32),
                        pltpu.VMEM((2, tm, d), F32),
                        pltpu.VMEM((2, tm, d), F32),
                        pltpu.SemaphoreType.DMA((2,)),
                        pltpu.SemaphoreType.DMA((2,)),
                        pltpu.SemaphoreType.DMA((2,))])
    return pl.pallas_call(
        _moe_kernel,
        out_shape=jax.ShapeDtypeStruct((t + 2 * tm, d), F32),
        grid_spec=grid_spec,
        compiler_params=_cparams("arbitrary"),
        name="moe_ffn",
    )(tile_ea, tile_eb, n_valid.reshape(1), gidx, sidx, h2, gates, wgu, wgu, wd, wd)


def _qkv_kernel(*refs, rope, cache_out):
    x1_ref, y_ref, mod0_ref, mod1_ref, g_ref, w_ref = refs[:6]
    pos = 6
    if rope:
        cos_ref, sa_ref, sb_ref = refs[pos:pos + 3]
        pos += 3
    x2_ref, q_ref, k_ref, v_ref = refs[pos:pos + 4]
    pos += 4
    m1 = mod1_ref[0]
    x2 = x1_ref[0] + _mod_part(mod0_ref[0], 5) * y_ref[...]
    x2_ref[0] = x2
    h = _modulate(x2, g_ref[...], _mod_part(m1, 0), _mod_part(m1, 1)).astype(BF16)
    qkv = _dot(h, w_ref[...])
    d = D_MODEL
    if cache_out:
        kc_ref, vc_ref = refs[pos:pos + 2]
        kc_ref[...] = qkv[:, d:2 * d]
        vc_ref[...] = qkv[:, 2 * d:]
    v_ref[...] = qkv[:, 2 * d:].astype(BF16)
    scale = HEAD_DIM ** -0.5
    for hd in range(N_HEADS):
        for base, ref, mul in ((0, q_ref, scale), (d, k_ref, 1.0)):
            sl = slice(base + hd * VALUE_DIM, base + (hd + 1) * VALUE_DIM)
            t = qkv[:, sl]
            if rope:
                t = (t * cos_ref[...] + pltpu.roll(t, VALUE_DIM - HEAD_DIM // 4, 1) * sa_ref[...]
                     + pltpu.roll(t, HEAD_DIM // 4, 1) * sb_ref[...])
            if mul != 1.0:
                t = t * mul
            ref[:, hd * VALUE_DIM:(hd + 1) * VALUE_DIM] = t.astype(BF16)


def _qkv(x1, ymoe, mod0, mod1, g, w_qkv, rope_tabs, cache_out, tm):
    b, n, d = x1.shape
    nt = n // tm
    per_batch = mod0.shape[0] > 1
    rope = rope_tabs is not None
    t = b * n
    in_specs = [pl.BlockSpec((1, tm, d), lambda bi, i: (bi, i, 0)),
                pl.BlockSpec((tm, d), lambda bi, i: (bi * nt + i, 0)),
                pl.BlockSpec((1, 1, 6 * d), lambda bi, i: (bi if per_batch else 0, 0, 0)),
                pl.BlockSpec((1, 1, 6 * d), lambda bi, i: (bi if per_batch else 0, 0, 0)),
                pl.BlockSpec((1, d), lambda bi, i: (0, 0)),
                pl.BlockSpec((d, 3 * d), lambda bi, i: (0, 0))]
    args = [x1, ymoe, mod0, mod1, g, w_qkv]
    if rope:
        in_specs += [pl.BlockSpec((tm, VALUE_DIM), lambda bi, i: (i, 0))] * 3
        args += list(rope_tabs)
    tok_spec = pl.BlockSpec((tm, d), lambda bi, i: (bi * nt + i, 0))
    out_shape = [jax.ShapeDtypeStruct((b, n, d), F32)] + [jax.ShapeDtypeStruct((t, d), BF16)] * 3
    out_specs = [pl.BlockSpec((1, tm, d), lambda bi, i: (bi, i, 0)), tok_spec, tok_spec, tok_spec]
    if cache_out:
        out_shape += [jax.ShapeDtypeStruct((t, d), F32)] * 2
        out_specs += [tok_spec, tok_spec]
    return pl.pallas_call(
        functools.partial(_qkv_kernel, rope=rope, cache_out=cache_out),
        out_shape=tuple(out_shape),
        grid=(b, nt),
        in_specs=in_specs,
        out_specs=tuple(out_specs),
        compiler_params=_cparams("arbitrary", "arbitrary"),
        name="qkv",
    )(*args)


def _attn_kernel(*refs, lam_init, has_cache):
    q_ref, k_ref, v_ref = refs[:3]
    pos = 3
    if has_cache:
        kc_ref, vc_ref = refs[3:5]
        pos = 5
    lam_ref, sg_ref, o_ref = refs[pos:pos + 3]
    lp = lam_ref[...]
    lam = (jnp.exp(jnp.sum(lp[0:1] * lp[1:2], axis=-1, keepdims=True))
           - jnp.exp(jnp.sum(lp[2:3] * lp[3:4], axis=-1, keepdims=True)) + lam_init)
    q = q_ref[...]
    first = lax.broadcasted_iota(I32, q.shape, 1) < HEAD_DIM
    zero = jnp.zeros_like(q)
    qs = (jnp.where(first, q, zero), jnp.where(first, zero, q))
    keys = [k_ref[...]] + ([kc_ref[...]] if has_cache else [])
    vals = [v_ref[...]] + ([vc_ref[...]] if has_cache else [])
    acc = None
    for which, qh in enumerate(qs):
        s = [lax.dot_general(qh, k, _NT, preferred_element_type=F32) for k in keys]
        m = s[0].max(axis=-1, keepdims=True)
        for t in s[1:]:
            m = jnp.maximum(m, t.max(axis=-1, keepdims=True))
        e = [jnp.exp(t - m) for t in s]
        den = e[0].sum(axis=-1, keepdims=True)
        for t in e[1:]:
            den = den + t.sum(axis=-1, keepdims=True)
        w = (1.0 / den) if which == 0 else (-lam / den)
        p = [t * w for t in e]
        acc = p if acc is None else [a + b for a, b in zip(acc, p)]
    o = None
    for a, v in zip(acc, vals):
        t = _dot(a.astype(BF16), v)
        o = t if o is None else o + t
    ms = jnp.mean(o * o, axis=-1, keepdims=True)
    o = (o * lax.rsqrt(ms + SUBLN_EPS)) * sg_ref[...] * (1.0 - lam_init)
    o_ref[...] = o.astype(BF16)


def _attention(q, k, v, kc, vc, lam_params, subln_g, b, n, lam_init, tq):
    t, d = q.shape
    nq = n // tq
    has_cache = kc is not None
    in_specs = [pl.BlockSpec((tq, VALUE_DIM), lambda bi, h, i: (bi * nq + i, h)),
                pl.BlockSpec((n, VALUE_DIM), lambda bi, h, i: (bi, h)),
                pl.BlockSpec((n, VALUE_DIM), lambda bi, h, i: (bi, h))]
    args = [q, k, v]
    if has_cache:
        p = kc.shape[0] // b
        in_specs += [pl.BlockSpec((p, VALUE_DIM), lambda bi, h, i: (bi, h))] * 2
        args += [kc, vc]
    in_specs += [pl.BlockSpec((4, HEAD_DIM), lambda bi, h, i: (0, 0)),
                 pl.BlockSpec((1, VALUE_DIM), lambda bi, h, i: (0, 0))]
    args += [lam_params, subln_g]
    return pl.pallas_call(
        functools.partial(_attn_kernel, lam_init=lam_init, has_cache=has_cache),
        out_shape=jax.ShapeDtypeStruct((t, d), BF16),
        grid=(b, N_HEADS, nq),
        in_specs=in_specs,
        out_specs=pl.BlockSpec((tq, VALUE_DIM), lambda bi, h, i: (bi * nq + i, h)),
        compiler_params=_cparams("arbitrary", "arbitrary", "arbitrary"),
        name="diff_attention",
    )(*args)


def _final_kernel(x_ref, y_ref, mod_ref, g_ref, o_ref):
    x = x_ref[0] + _mod_part(mod_ref[0], 5) * y_ref[...]
    ms = jnp.mean(x * x, axis=-1, keepdims=True)
    o_ref[0] = (x * lax.rsqrt(ms + NORM_EPS)) * g_ref[...]


def _final(x, ymoe, mod, g, tm):
    b, n, d = x.shape
    nt = n // tm
    per_batch = mod.shape[0] > 1
    return pl.pallas_call(
        _final_kernel,
        out_shape=jax.ShapeDtypeStruct((b, n, d), F32),
        grid=(b, nt),
        in_specs=[pl.BlockSpec((1, tm, d), lambda bi, i: (bi, i, 0)),
                  pl.BlockSpec((tm, d), lambda bi, i: (bi * nt + i, 0)),
                  pl.BlockSpec((1, 1, 6 * d), lambda bi, i: (bi if per_batch else 0, 0, 0)),
                  pl.BlockSpec((1, d), lambda bi, i: (0, 0))],
        out_specs=pl.BlockSpec((1, tm, d), lambda bi, i: (bi, i, 0)),
        compiler_params=_cparams("arbitrary", "arbitrary"),
        name="final_norm",
    )(x, ymoe, mod, g)


def _channel_dft_table():
    c = jnp.arange(GROUP_DIM, dtype=I32)
    ang = ((c[:, None] * c[None, :]) % GROUP_DIM).astype(F32) * (2.0 * math.pi / GROUP_DIM)
    tab = jnp.concatenate([jnp.cos(ang), jnp.sin(ang)], axis=1) * (GROUP_DIM ** -0.5)
    return tab.astype(BF16)


def _position_dft_table(n):
    nb = 64
    na = n // nb
    k = jnp.arange(n, dtype=I32)[:, None]
    a = jnp.arange(na, dtype=I32)[None, :]
    bb = jnp.arange(nb, dtype=I32)[None, :]
    ang_a = ((k * a) % na).astype(F32) * (2.0 * math.pi / na)
    ang_b = ((k * bb) % n).astype(F32) * (2.0 * math.pi / n)
    ca, sa = jnp.cos(ang_a)[:, :, None], jnp.sin(ang_a)[:, :, None]
    cb, sb = jnp.cos(ang_b)[:, None, :], jnp.sin(ang_b)[:, None, :]
    cos = (ca * cb - sa * sb).reshape(n, n)
    sin = (sa * cb + ca * sb).reshape(n, n)
    return (jnp.concatenate([cos, -sin], axis=1) * (n ** -0.5)).astype(BF16)


def _pool_tables(n):
    win = min(2 * POOL_TILE, n)

    def tab(t0, s0):
        t = t0 + jnp.arange(POOL_TILE, dtype=I32)[:, None]
        s = s0 + jnp.arange(win, dtype=I32)[None, :]
        rows = []
        for w in POOL_WINDOWS:
            lo = jnp.clip(t - w // 2, 0, n)
            hi = jnp.clip(t + w // 2, 0, n)
            cnt = jnp.maximum(hi - lo, 1).astype(F32)
            rows.append(jnp.where((s >= lo) & (s < hi), 1.0 / cnt, 0.0) - (s == t).astype(F32))
        return jnp.stack(rows)

    half = (win - POOL_TILE) // 2
    return jnp.stack([tab(0, 0), tab(POOL_TILE, POOL_TILE - half), tab(n - POOL_TILE, n - win)]).astype(BF16)


def _rope_tables(n):
    rows = n // GRID_W
    r = jnp.repeat(jnp.arange(rows, dtype=F32), GRID_W)
    c = jnp.tile(jnp.arange(GRID_W, dtype=F32), rows)
    quarter = HEAD_DIM // 4
    inv_freq = 1.0 / (ROPE_THETA ** (jnp.arange(quarter, dtype=F32) / quarter))
    ang = jnp.concatenate([r[:, None] * inv_freq] * 2 + [c[:, None] * inv_freq] * 2, axis=1)
    ang = jnp.concatenate([ang, ang], axis=1)
    first = (jnp.arange(VALUE_DIM) % (2 * quarter)) < quarter
    sin = jnp.sin(ang)
    return jnp.cos(ang), jnp.where(first, -sin, 0.0), jnp.where(first, 0.0, sin)


def _stream(x, mod, cache, w, rope):
    b, n, d = x.shape
    tm = min(512, n)
    t = b * n
    mod0, mod1 = mod[0], mod[1]

    ab, up = _inproj(x, mod0, w["norm1_g"][0:1], w["w_in"], w["cs_tab"], tm)
    gtab = _position_dft_table(n)
    cols = b * D_FOURIER
    ya = _posdft(gtab, ab.reshape(2 * n, cols), w["w_fourier"], min(1024, n), min(2048, cols), min(1024, 2 * n))
    yb = _pool(up, _pool_tables(n), w["w_pool"], w["pool_scale"])
    lhs_specs = [pl.BlockSpec((tm, D_FOURIER), lambda bi, i: (i, bi)),
                 pl.BlockSpec((1, tm, d - D_FOURIER), lambda bi, i: (bi, i, 0))]
    x1, h2, route = _proj([ya, yb], lhs_specs, ((0, D_FOURIER), (D_FOURIER, d)), w["w_out_ab"], x, mod0,
                          w["norm2_g"][0:1], w["wr_t"], w["rb"], tm)
    ymoe = _moe(h2, route, w["wgu"][0], w["wd"][0])

    nt = n // tm
    outs = _qkv(x1, ymoe, mod0, mod1, w["norm1_g"][1:2], w["w_qkv"], _rope_tables(n) if rope else None,
                cache is None, tm)
    x2, q, k, v = outs[:4]
    lam_init = 0.8 - 0.6 * math.exp(-0.3 * 1)
    kc, vc = cache if cache is not None else (None, None)
    o = _attention(q, k, v, kc, vc, w["lam"], w["subln_g"], b, n, lam_init, min(256, n))
    o_spec = [pl.BlockSpec((tm, d), lambda bi, i: (bi * nt + i, 0))]
    x3, h2, route = _proj([o], o_spec, ((0, d),), w["w_o"], x2, mod1, w["norm2_g"][1:2], w["wr_t"], w["rb"], tm)
    ymoe = _moe(h2, route, w["wgu"][1], w["wd"][1])
    y = _final(x3, ymoe, mod1, w["final_g"], tm)
    return y, outs[4:]


def kernel(x_prompt, x_sample, cache_k, cache_v, c, c_ctx, norm1_g, norm2_g, final_norm_g, w_ada, b_ada, w_in_ab, w_fourier, w_pool, pool_scale, w_out_ab, w_qkv, lambda_q1, lambda_k1, lambda_q2, lambda_k2, subln_g, w_o, w_router, router_bias, w_gate, w_up, w_down):
    d = D_MODEL
    bs = x_sample.shape[0]
    bp, sp = x_prompt.shape[:2]
    assert w_ada.shape[0] == 2 and w_in_ab.shape[0] == 1 and w_qkv.shape[0] == 1

    cond_rows = 16
    cond = jnp.zeros((cond_rows, d), F32).at[:bs].set(c).at[bs].set(c_ctx)
    mod = _ada_params(cond, w_ada, b_ada)
    mod_s = mod[:, :bs].reshape(2, bs, 1, 6 * d)
    mod_p = mod[:, bs:bs + 1].reshape(2, 1, 1, 6 * d)

    w = dict(
        norm1_g=norm1_g, norm2_g=norm2_g, final_g=final_norm_g.reshape(1, d),
        w_in=w_in_ab[0].astype(BF16), cs_tab=_channel_dft_table(),
        w_fourier=w_fourier[0].astype(BF16), w_pool=w_pool[0].astype(BF16), pool_scale=pool_scale[0:1],
        w_out_ab=w_out_ab[0].astype(BF16), w_qkv=w_qkv[0].astype(BF16), w_o=w_o[0].astype(BF16),
        lam=jnp.concatenate([lambda_q1, lambda_k1, lambda_q2, lambda_k2], axis=0),
        subln_g=subln_g[0:1], wr_t=w_router.T, rb=router_bias.reshape(N_EXPERTS, 1),
        wgu=jnp.concatenate([w_gate, w_up], axis=-1).astype(BF16), wd=w_down.astype(BF16))

    p = cache_k.shape[2]
    kc = cache_k[:, 0].reshape(bs * p, d).astype(BF16)
    vc = cache_v[:, 0].reshape(bs * p, d).astype(BF16)

    y_prompt, (kn, vn) = _stream(x_prompt, mod_p, None, w, rope=False)
    y_sample, _ = _stream(x_sample, mod_s, (kc, vc), w, rope=True)
    new_cache_k = kn.reshape(bp, 1, sp, N_HEADS, 2 * HEAD_DIM)
    new_cache_v = vn.reshape(bp, 1, sp, N_HEADS, VALUE_DIM)
    return (y_prompt, y_sample, new_cache_k, new_cache_v)
```

```python
import functools
import math

import jax
import jax.numpy as jnp
from jax import lax
from jax.experimental import pallas as pl
from jax.experimental.pallas import tpu as pltpu

F32 = jnp.float32
BF16 = jnp.bfloat16
I32 = jnp.int32

D_MODEL = 1024
GRID_W = 64
D_FOURIER = 512
GROUP_DIM = 128
N_GROUPS = 4
POOL_WINDOWS = (2, 4, 8, 16)
N_HEADS = 8
HEAD_DIM = 64
VALUE_DIM = 128
ROPE_THETA = 10000.0
N_EXPERTS = 16
EXPERTS_PER_GROUP = 4
PAIRS_PER_GROUP = 6
N_CLASSES = 24
D_EXPERT = 512
NORM_EPS = 1e-6
SUBLN_EPS = 1e-5

V7X_VMEM_BYTES = 64 * 1024 * 1024
VMEM_LIMIT = V7X_VMEM_BYTES * 3 // 4
LANES = 128
MOE_TILE = 256
POOL_TILE = 256
ROUTE_ROWS = 8
CLASS_ROWS = 32
REC_ROWS = 8
INVPERM_CHUNK = 4096
Q_SCALE = HEAD_DIM ** -0.5 * math.log2(math.e)

_NT = (((1,), (1,)), ((), ()))


def _cparams(*sem):
    return pltpu.CompilerParams(dimension_semantics=sem, vmem_limit_bytes=VMEM_LIMIT)


def _dot(a, b):
    return jnp.dot(a, b, preferred_element_type=F32)


def _split_bf16(x):
    hi = x.astype(BF16)
    lo = (x - hi.astype(F32)).astype(BF16)
    return hi, lo


def _modulate(x, g, shift, scale):
    ms = jnp.mean(x * x, axis=-1, keepdims=True)
    y = x * lax.rsqrt(ms + NORM_EPS)
    return (y * g) * (1.0 + scale) + shift


def _mod_part(m, k):
    return m[:, k * D_MODEL:(k + 1) * D_MODEL]


def _ada_kernel(c_ref, w_ref, b_ref, o_ref):
    c = c_ref[...]
    hi, lo = _split_bf16(c * jax.nn.sigmoid(c))
    whi, wlo = _split_bf16(w_ref[0])
    o_ref[0] = _dot(hi, whi) + _dot(lo, whi) + _dot(hi, wlo) + b_ref[0]


def _ada_params(cond, w_ada, b_ada):
    n_layers, d, d6 = w_ada.shape
    r = cond.shape[0]
    tn = d6 // 4
    return pl.pallas_call(
        _ada_kernel,
        out_shape=jax.ShapeDtypeStruct((n_layers, r, d6), F32),
        grid=(n_layers, d6 // tn),
        in_specs=[pl.BlockSpec((r, d), lambda l, j: (0, 0)),
                  pl.BlockSpec((1, d, tn), lambda l, j: (l, 0, j)),
                  pl.BlockSpec((1, 1, tn), lambda l, j: (l, 0, j))],
        out_specs=pl.BlockSpec((1, r, tn), lambda l, j: (l, 0, j)),
        compiler_params=_cparams("arbitrary", "arbitrary"),
        name="ada_params",
    )(cond, w_ada, b_ada.reshape(n_layers, 1, d6))


def _inproj_kernel(x_ref, mod_ref, g_ref, w_ref, cs_ref, ab_ref, up_ref):
    m = mod_ref[0]
    h = _modulate(x_ref[0], g_ref[...], _mod_part(m, 0), _mod_part(m, 1)).astype(BF16)
    u = _dot(h, w_ref[...])
    for g in range(N_GROUPS):
        sl = slice(g * GROUP_DIM, (g + 1) * GROUP_DIM)
        cs = _dot(u[:, sl].astype(BF16), cs_ref[...])
        ab_ref[0, :, sl] = cs[:, :GROUP_DIM].astype(BF16)
        ab_ref[1, :, sl] = cs[:, GROUP_DIM:].astype(BF16)
    up_ref[0] = u[:, D_FOURIER:].astype(BF16)


def _inproj(x, mod, g, w_in, cs_tab, tm):
    b, n, d = x.shape
    per_batch = mod.shape[0] > 1
    return pl.pallas_call(
        _inproj_kernel,
        out_shape=(jax.ShapeDtypeStruct((2, n, b * D_FOURIER), BF16),
                   jax.ShapeDtypeStruct((b, n, d - D_FOURIER), BF16)),
        grid=(b, n // tm),
        in_specs=[pl.BlockSpec((1, tm, d), lambda bi, i: (bi, i, 0)),
                  pl.BlockSpec((1, 1, 6 * d), lambda bi, i: (bi if per_batch else 0, 0, 0)),
                  pl.BlockSpec((1, d), lambda bi, i: (0, 0)),
                  pl.BlockSpec((d, d), lambda bi, i: (0, 0)),
                  pl.BlockSpec((GROUP_DIM, 2 * GROUP_DIM), lambda bi, i: (0, 0))],
        out_specs=(pl.BlockSpec((2, tm, D_FOURIER), lambda bi, i: (0, i, bi)),
                   pl.BlockSpec((1, tm, d - D_FOURIER), lambda bi, i: (bi, i, 0))),
        compiler_params=_cparams("arbitrary", "arbitrary"),
        name="inproj",
    )(x, mod, g, w_in, cs_tab)


def _posdft_kernel(g_ref, ab_ref, wf_ref, o_ref, acc_ref, *, nk, tn):
    k = pl.program_id(2)

    @pl.when(k == 0)
    def _():
        acc_ref[...] = jnp.zeros_like(acc_ref)

    acc_ref[...] += _dot(g_ref[...], ab_ref[...])

    @pl.when(k == nk - 1)
    def _():
        for j in range(tn // GROUP_DIM):
            sl = slice(j * GROUP_DIM, (j + 1) * GROUP_DIM)
            f = acc_ref[:, sl].astype(BF16)
            o_ref[:, sl] = _dot(f, wf_ref[j % N_GROUPS]).astype(BF16)


def _posdft(gtab, ab, w_fourier, tm, tn, tk):
    n, k2 = gtab.shape
    cols = ab.shape[1]
    nk = k2 // tk
    return pl.pallas_call(
        functools.partial(_posdft_kernel, nk=nk, tn=tn),
        out_shape=jax.ShapeDtypeStruct((n, cols), BF16),
        grid=(n // tm, cols // tn, nk),
        in_specs=[pl.BlockSpec((tm, tk), lambda i, j, k: (i, k)),
                  pl.BlockSpec((tk, tn), lambda i, j, k: (k, j)),
                  pl.BlockSpec((N_GROUPS, GROUP_DIM, GROUP_DIM), lambda i, j, k: (0, 0, 0))],
        out_specs=pl.BlockSpec((tm, tn), lambda i, j, k: (i, j)),
        scratch_shapes=[pltpu.VMEM((tm, tn), F32)],
        compiler_params=_cparams("arbitrary", "arbitrary", "arbitrary"),
        name="posdft",
    )(gtab, ab, w_fourier)


def _pool_kernel(up_ref, w_ref, wp_ref, ps_ref, o_ref, *, n, win):
    j = pl.program_id(1)
    start = pl.multiple_of(jnp.clip(j * POOL_TILE - (win - POOL_TILE) // 2, 0, n - win), LANES)
    for g in range(N_GROUPS):
        sl = slice(g * GROUP_DIM, (g + 1) * GROUP_DIM)
        xg = up_ref[0, pl.ds(start, win), sl]
        mixed = _dot(w_ref[0, g], xg).astype(BF16)
        o_ref[0, :, sl] = (_dot(mixed, wp_ref[g]) * ps_ref[:, sl]).astype(BF16)


def _pool(up, wtab, w_pool, pool_scale):
    b, n, dp = up.shape
    win = wtab.shape[-1]
    n_t = n // POOL_TILE

    def widx(bi, j):
        return (jnp.where(j == 0, 0, jnp.where(j == n_t - 1, 2, 1)), 0, 0, 0)

    return pl.pallas_call(
        functools.partial(_pool_kernel, n=n, win=win),
        out_shape=jax.ShapeDtypeStruct((b, n, dp), BF16),
        grid=(b, n_t),
        in_specs=[pl.BlockSpec((1, n, dp), lambda bi, j: (bi, 0, 0)),
                  pl.BlockSpec((1, N_GROUPS, POOL_TILE, win), widx),
                  pl.BlockSpec((N_GROUPS, GROUP_DIM, GROUP_DIM), lambda bi, j: (0, 0, 0)),
                  pl.BlockSpec((1, dp), lambda bi, j: (0, 0))],
        out_specs=pl.BlockSpec((1, POOL_TILE, dp), lambda bi, j: (bi, j, 0)),
        compiler_params=_cparams("arbitrary", "arbitrary"),
        name="pool",
    )(up, wtab, w_pool, pool_scale)


def _route(h2, wr_ref, rb_ref, tri_ref, route_ref, hist_ref):
    hi, lo = _split_bf16(h2)
    whi, wlo = _split_bf16(wr_ref[...])
    lg = (lax.dot_general(whi, hi, _NT, preferred_element_type=F32)
          + lax.dot_general(whi, lo, _NT, preferred_element_type=F32)
          + lax.dot_general(wlo, hi, _NT, preferred_element_type=F32))
    sel = jax.nn.sigmoid(lg) + rb_ref[...]
    vrow = [sel[e:e + 1, :] for e in range(N_EXPERTS)]

    best_score = None
    best_group = None
    for g in range(N_EXPERTS // EXPERTS_PER_GROUP):
        a, b, c, d = vrow[4 * g:4 * g + 4]
        hi1, lo1, hi2, lo2 = jnp.maximum(a, b), jnp.minimum(a, b), jnp.maximum(c, d), jnp.minimum(c, d)
        score = jnp.maximum(hi1, hi2) + jnp.maximum(jnp.minimum(hi1, hi2), jnp.maximum(lo1, lo2))
        if g == 0:
            best_score, best_group = score, jnp.zeros_like(score)
        else:
            better = score > best_score
            best_group = jnp.where(better, float(g), best_group)
            best_score = jnp.where(better, score, best_score)

    def pick(rows, j):
        out = rows[j]
        for g in range(1, N_EXPERTS // EXPERTS_PER_GROUP):
            out = jnp.where(best_group == float(g), rows[4 * g + j], out)
        return out

    v = [pick(vrow, j) for j in range(EXPERTS_PER_GROUP)]
    i0, m0 = jnp.zeros_like(v[0]), v[0]
    for j in range(1, EXPERTS_PER_GROUP):
        better = v[j] > m0
        i0 = jnp.where(better, float(j), i0)
        m0 = jnp.where(better, v[j], m0)
    i1, m1 = jnp.full_like(v[0], -1.0), jnp.full_like(v[0], -jnp.inf)
    for j in range(EXPERTS_PER_GROUP):
        better = (i0 != float(j)) & ((v[j] > m1) | (i1 < 0.0))
        i1 = jnp.where(better, float(j), i1)
        m1 = jnp.where(better, v[j], m1)

    lo_i, hi_i = jnp.minimum(i0, i1), jnp.maximum(i0, i1)
    pair = jnp.where(lo_i == 0.0, hi_i - 1.0, jnp.where(lo_i == 1.0, hi_i + 1.0, 5.0))
    cls = best_group * float(PAIRS_PER_GROUP) + pair
    tm = cls.shape[1]

    onehot = (lax.broadcasted_iota(I32, (LANES, tm), 0).astype(F32) == cls).astype(BF16)
    before = _dot(onehot[:CLASS_ROWS], tri_ref[...])
    rank = jnp.sum(onehot[:CLASS_ROWS].astype(F32) * before, axis=0, keepdims=True)
    hist_ref[...] = lax.dot_general(jnp.ones((8, tm), BF16), onehot, _NT, preferred_element_type=F32)
    route_ref[0:1, :] = cls
    route_ref[1:2, :] = rank
    route_ref[2:ROUTE_ROWS, :] = jnp.zeros((ROUTE_ROWS - 2, tm), F32)


def _proj_kernel(*refs, n_lhs, splits):
    lhs = refs[:n_lhs]
    (w_ref, x_ref, mod_ref, g_ref, wr_ref, rb_ref, tri_ref,
     x1_ref, rec_ref, route_ref, hist_ref) = refs[n_lhs:]
    y = None
    for r, (lo, hi) in zip(lhs, splits):
        a = r[0] if len(r.shape) == 3 else r[...]
        t = _dot(a, w_ref[lo:hi, :])
        y = t if y is None else y + t
    m = mod_ref[0]
    x1 = x_ref[0] + _mod_part(m, 2) * y
    x1_ref[0] = x1
    h2 = _modulate(x1, g_ref[...], _mod_part(m, 3), _mod_part(m, 4))
    _route(h2, wr_ref, rb_ref, tri_ref, route_ref, hist_ref)
    tm = h2.shape[0]
    for s in range(REC_ROWS):
        rec_ref[pl.ds(s, tm, stride=REC_ROWS), :] = h2[:, s * LANES:(s + 1) * LANES]


def _proj(lhs, lhs_specs, splits, w, x, mod, g2, wr_t, rb, tri, tm):
    b, n, d = x.shape
    nt = n // tm
    per_batch = mod.shape[0] > 1
    n_lhs = len(lhs)
    return pl.pallas_call(
        functools.partial(_proj_kernel, n_lhs=n_lhs, splits=splits),
        out_shape=(jax.ShapeDtypeStruct((b, n, d), F32),
                   jax.ShapeDtypeStruct((b * n * REC_ROWS, LANES), F32),
                   jax.ShapeDtypeStruct((ROUTE_ROWS, b * n), F32),
                   jax.ShapeDtypeStruct((b * nt * 8, LANES), F32)),
        grid=(b, nt),
        in_specs=list(lhs_specs) + [
            pl.BlockSpec((d, d), lambda bi, i: (0, 0)),
            pl.BlockSpec((1, tm, d), lambda bi, i: (bi, i, 0)),
            pl.BlockSpec((1, 1, 6 * d), lambda bi, i: (bi if per_batch else 0, 0, 0)),
            pl.BlockSpec((1, d), lambda bi, i: (0, 0)),
            pl.BlockSpec((N_EXPERTS, d), lambda bi, i: (0, 0)),
            pl.BlockSpec((N_EXPERTS, 1), lambda bi, i: (0, 0)),
            pl.BlockSpec((tm, tm), lambda bi, i: (0, 0))],
        out_specs=(pl.BlockSpec((1, tm, d), lambda bi, i: (bi, i, 0)),
                   pl.BlockSpec((tm * REC_ROWS, LANES), lambda bi, i: (bi * nt + i, 0)),
                   pl.BlockSpec((ROUTE_ROWS, tm), lambda bi, i: (0, bi * nt + i)),
                   pl.BlockSpec((8, LANES), lambda bi, i: (bi * nt + i, 0))),
        compiler_params=_cparams("arbitrary", "arbitrary"),
        name="proj_route",
    )(*lhs, w, x, mod, g2, wr_t, rb, tri)


def _invperm_kernel(dest_hbm, init_hbm, out_ref, dest_smem, sem):
    n_tok = dest_hbm.shape[0]
    chunk = dest_smem.shape[0]
    init = pltpu.make_async_copy(init_hbm, out_ref, sem.at[0])
    init.start()
    init.wait()
    for c in range(n_tok // chunk):
        cp = pltpu.make_async_copy(dest_hbm.at[pl.ds(c * chunk, chunk)], dest_smem, sem.at[1])
        cp.start()
        cp.wait()

        def body(j, carry):
            out_ref[dest_smem[j]] = c * chunk + j
            return carry
        lax.fori_loop(0, chunk, body, 0, unroll=8)


def _invperm(dest, n_slots):
    n_tok = dest.shape[0]
    chunk = min(INVPERM_CHUNK, n_tok)
    assert n_tok % chunk == 0
    return pl.pallas_call(
        _invperm_kernel,
        out_shape=jax.ShapeDtypeStruct((n_slots,), I32),
        in_specs=[pl.BlockSpec(memory_space=pl.ANY), pl.BlockSpec(memory_space=pl.ANY)],
        out_specs=pl.BlockSpec(memory_space=pltpu.SMEM),
        scratch_shapes=[pltpu.SMEM((chunk,), I32), pltpu.SemaphoreType.DMA((2,))],
        name="moe_invperm",
    )(dest, jnp.full((n_slots,), -1, I32))


def _moe_kernel(ea_ref, eb_ref, nv_ref, tok_hbm, rec_hbm, wr_a, wr_b, wgu_a, wgu_b, wd_a, wd_b,
                y_hbm, tok_smem, xbuf, ybuf, sem_idx, sem_g, sem_s):
    del ea_ref, eb_ref
    i = pl.program_id(0)
    n_valid = nv_ref[0]
    slot = i % 2
    tm = MOE_TILE
    rows = tm * REC_ROWS
    n_tok = y_hbm.shape[0] // REC_ROWS - 2 * tm

    def idx_copy(t):
        return pltpu.make_async_copy(tok_hbm.at[pl.ds(t * tm, tm)], tok_smem.at[pl.ds((t % 3) * tm, tm)],
                                     sem_idx.at[t % 3])

    def gather_rows(t):
        sl = t % 2

        def body(r, carry):
            tok = jnp.maximum(tok_smem[(t % 3) * tm + r], 0)
            src = pl.multiple_of(tok * REC_ROWS, REC_ROWS)
            dst = pl.multiple_of((sl * tm + r) * REC_ROWS, REC_ROWS)
            pltpu.make_async_copy(rec_hbm.at[pl.ds(src, REC_ROWS)], xbuf.at[pl.ds(dst, REC_ROWS)],
                                  sem_g.at[sl]).start()
            return carry
        lax.fori_loop(0, tm, body, 0, unroll=8)

    def scatter_rows(t):
        sl = t % 2

        def body(r, carry):
            tok = tok_smem[(t % 3) * tm + r]
            tok = jnp.where(tok >= 0, tok, n_tok + sl * tm + r)
            src = pl.multiple_of((sl * tm + r) * REC_ROWS, REC_ROWS)
            dst = pl.multiple_of(tok * REC_ROWS, REC_ROWS)
            pltpu.make_async_copy(ybuf.at[pl.ds(src, REC_ROWS)], y_hbm.at[pl.ds(dst, REC_ROWS)],
                                  sem_s.at[sl]).start()
            return carry
        lax.fori_loop(0, tm, body, 0, unroll=8)

    def wait_gather(sl):
        pltpu.make_async_copy(rec_hbm.at[pl.ds(0, rows)], xbuf.at[pl.ds(sl * rows, rows)], sem_g.at[sl]).wait()

    def wait_scatter(sl):
        pltpu.make_async_copy(ybuf.at[pl.ds(sl * rows, rows)], y_hbm.at[pl.ds(0, rows)], sem_s.at[sl]).wait()

    @pl.when(i == 0)
    def _():
        idx_copy(0).start()
        idx_copy(0).wait()
        gather_rows(0)

        @pl.when(n_valid > 1)
        def _():
            idx_copy(1).start()
        ybuf[...] = jnp.zeros_like(ybuf)
        for sl in range(2):
            pltpu.make_async_copy(ybuf.at[pl.ds(sl * rows, rows)],
                                  y_hbm.at[pl.ds((n_tok + sl * tm) * REC_ROWS, rows)], sem_s.at[sl]).start()
        for sl in range(2):
            wait_scatter(sl)

    @pl.when(i + 1 < n_valid)
    def _():
        idx_copy(i + 1).wait()
        gather_rows(i + 1)

    @pl.when(i + 2 < n_valid)
    def _():
        idx_copy(i + 2).start()

    @pl.when(i < n_valid)
    def _():
        wait_gather(slot)

        @pl.when(i >= 2)
        def _():
            wait_scatter(slot)

        base = pl.multiple_of(slot * rows, rows)
        h = jnp.concatenate([xbuf[pl.ds(base + s, tm, stride=REC_ROWS), :] for s in range(REC_ROWS)], axis=1)
        x = h.astype(BF16)
        sa = jax.nn.sigmoid(jnp.sum(h * wr_a[0], axis=-1, keepdims=True))
        sb = jax.nn.sigmoid(jnp.sum(h * wr_b[0], axis=-1, keepdims=True))
        den = sa + sb

        def ffn(wgu, wd):
            gu = _dot(x, wgu[0])
            g, u = gu[:, :D_EXPERT], gu[:, D_EXPERT:]
            he = (g * jax.nn.sigmoid(g)) * u
            return _dot(he.astype(BF16), wd[0])

        y = (sa / den) * ffn(wgu_a, wd_a) + (sb / den) * ffn(wgu_b, wd_b)
        for s in range(REC_ROWS):
            ybuf[pl.ds(base + s, tm, stride=REC_ROWS), :] = y[:, s * LANES:(s + 1) * LANES]
        scatter_rows(i)

        @pl.when(i == n_valid - 1)
        def _():
            wait_scatter(slot)

            @pl.when(i >= 1)
            def _():
                wait_scatter(1 - slot)


def _moe(rec, route, hist, wr_rows, wgu, wd, tm_tok):
    t = route.shape[1]
    d = D_MODEL
    tm = MOE_TILE
    n_tiles = t // tm + N_CLASSES
    n_src = t // tm_tok

    cls = route[0].astype(I32).reshape(n_src, tm_tok)
    rank = route[1].astype(I32).reshape(n_src, tm_tok)
    tile_hist = hist.reshape(n_src, 8, LANES)[:, 0, :N_CLASSES].astype(I32)
    before = jnp.cumsum(tile_hist, axis=0) - tile_hist
    counts = jnp.sum(tile_hist, axis=0)
    padded = ((counts + tm - 1) // tm) * tm
    pend = jnp.cumsum(padded)
    base = (pend - padded)[None, :] + before
    onehot = cls[:, :, None] == jnp.arange(N_CLASSES, dtype=I32)[None, None, :]
    dest = (rank + jnp.sum(jnp.where(onehot, base[:, None, :], 0), axis=-1)).reshape(t)
    slot_tok = _invperm(dest, n_tiles * tm)

    n_valid = (pend[-1] // tm).astype(I32)
    tile_ids = jnp.arange(n_tiles, dtype=I32)
    tile_cls = jnp.sum((tile_ids[:, None] * tm >= pend[None, :]).astype(I32), axis=1)
    last_cls = jnp.sum(((n_valid - 1) * tm >= pend).astype(I32))
    tile_cls = jnp.where(tile_ids < n_valid, tile_cls, last_cls)
    pair = tile_cls % PAIRS_PER_GROUP
    pair_lo = jnp.where(pair < 3, 0, jnp.where(pair < 5, 1, 2))
    pair_hi = jnp.where(pair < 3, pair + 1, jnp.where(pair < 5, pair - 1, 3))
    group0 = (tile_cls // PAIRS_PER_GROUP) * EXPERTS_PER_GROUP
    tile_ea = group0 + pair_lo
    tile_eb = group0 + pair_hi

    grid_spec = pltpu.PrefetchScalarGridSpec(
        num_scalar_prefetch=3,
        grid=(n_tiles,),
        in_specs=[pl.BlockSpec(memory_space=pl.ANY),
                  pl.BlockSpec(memory_space=pl.ANY),
                  pl.BlockSpec((1, 1, d), lambda i, ea, eb, nv: (ea[i], 0, 0)),
                  pl.BlockSpec((1, 1, d), lambda i, ea, eb, nv: (eb[i], 0, 0)),
                  pl.BlockSpec((1, d, 2 * D_EXPERT), lambda i, ea, eb, nv: (ea[i], 0, 0)),
                  pl.BlockSpec((1, d, 2 * D_EXPERT), lambda i, ea, eb, nv: (eb[i], 0, 0)),
                  pl.BlockSpec((1, D_EXPERT, d), lambda i, ea, eb, nv: (ea[i], 0, 0)),
                  pl.BlockSpec((1, D_EXPERT, d), lambda i, ea, eb, nv: (eb[i], 0, 0))],
        out_specs=pl.BlockSpec(memory_space=pl.ANY),
        scratch_shapes=[pltpu.SMEM((3 * tm,), I32),
                        pltpu.VMEM((2 * tm * REC_ROWS, LANES), F32),
                        pltpu.VMEM((2 * tm * REC_ROWS, LANES), F32),
                        pltpu.SemaphoreType.DMA((3,)),
                        pltpu.SemaphoreType.DMA((2,)),
                        pltpu.SemaphoreType.DMA((2,))])
    return pl.pallas_call(
        _moe_kernel,
        out_shape=jax.ShapeDtypeStruct(((t + 2 * tm) * REC_ROWS, LANES), F32),
        grid_spec=grid_spec,
        compiler_params=_cparams("arbitrary"),
        name="moe_ffn",
    )(tile_ea, tile_eb, n_valid.reshape(1), slot_tok, rec, wr_rows, wr_rows, wgu, wgu, wd, wd)


def _token_rows(y_ref, tm):
    return jnp.concatenate([y_ref[pl.ds(s, tm, stride=REC_ROWS), :] for s in range(REC_ROWS)], axis=1)


def _qkv_kernel(*refs, rope, cache_out):
    x1_ref, y_ref, mod0_ref, mod1_ref, g_ref, w_ref = refs[:6]
    pos = 6
    if rope:
        cos_ref, sa_ref, sb_ref = refs[pos:pos + 3]
        pos += 3
    x2_ref, q_ref, k_ref, v_ref = refs[pos:pos + 4]
    pos += 4
    m1 = mod1_ref[0]
    x2 = x1_ref[0] + _mod_part(mod0_ref[0], 5) * _token_rows(y_ref, x1_ref.shape[1])
    x2_ref[0] = x2
    h = _modulate(x2, g_ref[...], _mod_part(m1, 0), _mod_part(m1, 1)).astype(BF16)
    qkv = _dot(h, w_ref[...])
    d = D_MODEL
    if cache_out:
        kc_ref, vc_ref = refs[pos:pos + 2]
        kc_ref[...] = qkv[:, d:2 * d]
        vc_ref[...] = qkv[:, 2 * d:]
    v_ref[...] = qkv[:, 2 * d:].astype(BF16)
    for hd in range(N_HEADS):
        for base, ref, mul in ((0, q_ref, Q_SCALE), (d, k_ref, 1.0)):
            sl = slice(base + hd * VALUE_DIM, base + (hd + 1) * VALUE_DIM)
            t = qkv[:, sl]
            if rope:
                t = (t * cos_ref[...] + pltpu.roll(t, VALUE_DIM - HEAD_DIM // 4, 1) * sa_ref[...]
                     + pltpu.roll(t, HEAD_DIM // 4, 1) * sb_ref[...])
            if mul != 1.0:
                t = t * mul
            ref[:, hd * VALUE_DIM:(hd + 1) * VALUE_DIM] = t.astype(BF16)


def _qkv(x1, ymoe, mod0, mod1, g, w_qkv, rope_tabs, cache_out, tm):
    b, n, d = x1.shape
    nt = n // tm
    per_batch = mod0.shape[0] > 1
    rope = rope_tabs is not None
    t = b * n
    in_specs = [pl.BlockSpec((1, tm, d), lambda bi, i: (bi, i, 0)),
                pl.BlockSpec((tm * REC_ROWS, LANES), lambda bi, i: (bi * nt + i, 0)),
                pl.BlockSpec((1, 1, 6 * d), lambda bi, i: (bi if per_batch else 0, 0, 0)),
                pl.BlockSpec((1, 1, 6 * d), lambda bi, i: (bi if per_batch else 0, 0, 0)),
                pl.BlockSpec((1, d), lambda bi, i: (0, 0)),
                pl.BlockSpec((d, 3 * d), lambda bi, i: (0, 0))]
    args = [x1, ymoe, mod0, mod1, g, w_qkv]
    if rope:
        in_specs += [pl.BlockSpec((tm, VALUE_DIM), lambda bi, i: (i, 0))] * 3
        args += list(rope_tabs)
    tok_spec = pl.BlockSpec((tm, d), lambda bi, i: (bi * nt + i, 0))
    out_shape = [jax.ShapeDtypeStruct((b, n, d), F32)] + [jax.ShapeDtypeStruct((t, d), BF16)] * 3
    out_specs = [pl.BlockSpec((1, tm, d), lambda bi, i: (bi, i, 0)), tok_spec, tok_spec, tok_spec]
    if cache_out:
        out_shape += [jax.ShapeDtypeStruct((t, d), F32)] * 2
        out_specs += [tok_spec, tok_spec]
    return pl.pallas_call(
        functools.partial(_qkv_kernel, rope=rope, cache_out=cache_out),
        out_shape=tuple(out_shape),
        grid=(b, nt),
        in_specs=in_specs,
        out_specs=tuple(out_specs),
        compiler_params=_cparams("arbitrary", "arbitrary"),
        name="qkv",
    )(*args)


def _attn_kernel(*refs, lam_init, has_cache):
    q_ref, k_ref, v_ref = refs[:3]
    pos = 3
    if has_cache:
        kc_ref, vc_ref = refs[3:5]
        pos = 5
    lam_ref, sg_ref, o_ref = refs[pos:pos + 3]
    lp = lam_ref[...]
    lam = (jnp.exp(jnp.sum(lp[0:1] * lp[1:2], axis=-1, keepdims=True))
           - jnp.exp(jnp.sum(lp[2:3] * lp[3:4], axis=-1, keepdims=True)) + lam_init)
    q = q_ref[...]
    first = lax.broadcasted_iota(I32, q.shape, 1) < HEAD_DIM
    zero = jnp.zeros_like(q)
    qs = (jnp.where(first, q, zero), jnp.where(first, zero, q))
    keys = [k_ref[...]] + ([kc_ref[...]] if has_cache else [])
    vals = [v_ref[...]] + ([vc_ref[...]] if has_cache else [])
    acc = None
    for which, qh in enumerate(qs):
        s = [lax.dot_general(qh, k, _NT, preferred_element_type=F32) for k in keys]
        m = s[0].max(axis=-1, keepdims=True)
        for t in s[1:]:
            m = jnp.maximum(m, t.max(axis=-1, keepdims=True))
        e = [jnp.exp2(t - m) for t in s]
        den = e[0].sum(axis=-1, keepdims=True)
        for t in e[1:]:
            den = den + t.sum(axis=-1, keepdims=True)
        w = (1.0 / den) if which == 0 else (-lam / den)
        p = [t * w for t in e]
        acc = p if acc is None else [a + b for a, b in zip(acc, p)]
    o = None
    for a, v in zip(acc, vals):
        t = _dot(a.astype(BF16), v)
        o = t if o is None else o + t
    ms = jnp.mean(o * o, axis=-1, keepdims=True)
    o = (o * lax.rsqrt(ms + SUBLN_EPS)) * sg_ref[...] * (1.0 - lam_init)
    o_ref[...] = o.astype(BF16)


def _attention(q, k, v, kc, vc, lam_params, subln_g, b, n, lam_init, tq):
    t, d = q.shape
    nq = n // tq
    has_cache = kc is not None
    in_specs = [pl.BlockSpec((tq, VALUE_DIM), lambda bi, h, i: (bi * nq + i, h)),
                pl.BlockSpec((n, VALUE_DIM), lambda bi, h, i: (bi, h)),
                pl.BlockSpec((n, VALUE_DIM), lambda bi, h, i: (bi, h))]
    args = [q, k, v]
    if has_cache:
        p = kc.shape[0] // b
        in_specs += [pl.BlockSpec((p, VALUE_DIM), lambda bi, h, i: (bi, h))] * 2
        args += [kc, vc]
    in_specs += [pl.BlockSpec((4, HEAD_DIM), lambda bi, h, i: (0, 0)),
                 pl.BlockSpec((1, VALUE_DIM), lambda bi, h, i: (0, 0))]
    args += [lam_params, subln_g]
    return pl.pallas_call(
        functools.partial(_attn_kernel, lam_init=lam_init, has_cache=has_cache),
        out_shape=jax.ShapeDtypeStruct((t, d), BF16),
        grid=(b, N_HEADS, nq),
        in_specs=in_specs,
        out_specs=pl.BlockSpec((tq, VALUE_DIM), lambda bi, h, i: (bi * nq + i, h)),
        compiler_params=_cparams("arbitrary", "arbitrary", "arbitrary"),
        name="diff_attention",
    )(*args)


def _final_kernel(x_ref, y_ref, mod_ref, g_ref, o_ref):
    x = x_ref[0] + _mod_part(mod_ref[0], 5) * _token_rows(y_ref, x_ref.shape[1])
    ms = jnp.mean(x * x, axis=-1, keepdims=True)
    o_ref[0] = (x * lax.rsqrt(ms + NORM_EPS)) * g_ref[...]


def _final(x, ymoe, mod, g, tm):
    b, n, d = x.shape
    nt = n // tm
    per_batch = mod.shape[0] > 1
    return pl.pallas_call(
        _final_kernel,
        out_shape=jax.ShapeDtypeStruct((b, n, d), F32),
        grid=(b, nt),
        in_specs=[pl.BlockSpec((1, tm, d), lambda bi, i: (bi, i, 0)),
                  pl.BlockSpec((tm * REC_ROWS, LANES), lambda bi, i: (bi * nt + i, 0)),
                  pl.BlockSpec((1, 1, 6 * d), lambda bi, i: (bi if per_batch else 0, 0, 0)),
                  pl.BlockSpec((1, d), lambda bi, i: (0, 0))],
        out_specs=pl.BlockSpec((1, tm, d), lambda bi, i: (bi, i, 0)),
        compiler_params=_cparams("arbitrary", "arbitrary"),
        name="final_norm",
    )(x, ymoe, mod, g)


def _channel_dft_table():
    c = jnp.arange(GROUP_DIM, dtype=I32)
    ang = ((c[:, None] * c[None, :]) % GROUP_DIM).astype(F32) * (2.0 * math.pi / GROUP_DIM)
    tab = jnp.concatenate([jnp.cos(ang), jnp.sin(ang)], axis=1) * (GROUP_DIM ** -0.5)
    return tab.astype(BF16)


def _position_dft_table(n):
    nb = 64
    na = n // nb
    k = jnp.arange(n, dtype=I32)[:, None]
    a = jnp.arange(na, dtype=I32)[None, :]
    bb = jnp.arange(nb, dtype=I32)[None, :]
    ang_a = ((k * a) % na).astype(F32) * (2.0 * math.pi / na)
    ang_b = ((k * bb) % n).astype(F32) * (2.0 * math.pi / n)
    ca, sa = jnp.cos(ang_a)[:, :, None], jnp.sin(ang_a)[:, :, None]
    cb, sb = jnp.cos(ang_b)[:, None, :], jnp.sin(ang_b)[:, None, :]
    cos = (ca * cb - sa * sb).reshape(n, n)
    sin = (sa * cb + ca * sb).reshape(n, n)
    return (jnp.concatenate([cos, -sin], axis=1) * (n ** -0.5)).astype(BF16)


def _pool_tables(n):
    win = min(2 * POOL_TILE, n)

    def tab(t0, s0):
        t = t0 + jnp.arange(POOL_TILE, dtype=I32)[:, None]
        s = s0 + jnp.arange(win, dtype=I32)[None, :]
        rows = []
        for w in POOL_WINDOWS:
            lo = jnp.clip(t - w // 2, 0, n)
            hi = jnp.clip(t + w // 2, 0, n)
            cnt = jnp.maximum(hi - lo, 1).astype(F32)
            rows.append(jnp.where((s >= lo) & (s < hi), 1.0 / cnt, 0.0) - (s == t).astype(F32))
        return jnp.stack(rows)

    half = (win - POOL_TILE) // 2
    return jnp.stack([tab(0, 0), tab(POOL_TILE, POOL_TILE - half), tab(n - POOL_TILE, n - win)]).astype(BF16)


def _rope_tables(n):
    rows = n // GRID_W
    r = jnp.repeat(jnp.arange(rows, dtype=F32), GRID_W)
    c = jnp.tile(jnp.arange(GRID_W, dtype=F32), rows)
    quarter = HEAD_DIM // 4
    inv_freq = 1.0 / (ROPE_THETA ** (jnp.arange(quarter, dtype=F32) / quarter))
    ang = jnp.concatenate([r[:, None] * inv_freq] * 2 + [c[:, None] * inv_freq] * 2, axis=1)
    ang = jnp.concatenate([ang, ang], axis=1)
    first = (jnp.arange(VALUE_DIM) % (2 * quarter)) < quarter
    sin = jnp.sin(ang)
    return jnp.cos(ang), jnp.where(first, -sin, 0.0), jnp.where(first, 0.0, sin)


def _stream(x, mod, cache, w, rope):
    b, n, d = x.shape
    tm = min(512, n)
    t = b * n
    mod0, mod1 = mod[0], mod[1]

    ab, up = _inproj(x, mod0, w["norm1_g"][0:1], w["w_in"], w["cs_tab"], tm)
    gtab = _position_dft_table(n)
    cols = b * D_FOURIER
    ya = _posdft(gtab, ab.reshape(2 * n, cols), w["w_fourier"], min(1024, n), min(2048, cols), min(1024, 2 * n))
    yb = _pool(up, _pool_tables(n), w["w_pool"], w["pool_scale"])
    lhs_specs = [pl.BlockSpec((tm, D_FOURIER), lambda bi, i: (i, bi)),
                 pl.BlockSpec((1, tm, d - D_FOURIER), lambda bi, i: (bi, i, 0))]
    tri = jnp.triu(jnp.ones((tm, tm), F32), k=1).astype(BF16)
    x1, rec, route, hist = _proj([ya, yb], lhs_specs, ((0, D_FOURIER), (D_FOURIER, d)), w["w_out_ab"], x, mod0,
                                 w["norm2_g"][0:1], w["wr_t"], w["rb"], tri, tm)
    wr_rows = w["wr_t"].reshape(N_EXPERTS, 1, d)
    ymoe = _moe(rec, route, hist, wr_rows, w["wgu"][0], w["wd"][0], tm)

    nt = n // tm
    outs = _qkv(x1, ymoe, mod0, mod1, w["norm1_g"][1:2], w["w_qkv"], _rope_tables(n) if rope else None,
                cache is None, tm)
    x2, q, k, v = outs[:4]
    lam_init = 0.8 - 0.6 * math.exp(-0.3 * 1)
    kc, vc = cache if cache is not None else (None, None)
    o = _attention(q, k, v, kc, vc, w["lam"], w["subln_g"], b, n, lam_init, min(256, n))
    o_spec = [pl.BlockSpec((tm, d), lambda bi, i: (bi * nt + i, 0))]
    x3, rec, route, hist = _proj([o], o_spec, ((0, d),), w["w_o"], x2, mod1, w["norm2_g"][1:2], w["wr_t"],
                                 w["rb"], tri, tm)
    ymoe = _moe(rec, route, hist, wr_rows, w["wgu"][1], w["wd"][1], tm)
    y = _final(x3, ymoe, mod1, w["final_g"], tm)
    return y, outs[4:]


def kernel(x_prompt, x_sample, cache_k, cache_v, c, c_ctx, norm1_g, norm2_g, final_norm_g, w_ada, b_ada, w_in_ab, w_fourier, w_pool, pool_scale, w_out_ab, w_qkv, lambda_q1, lambda_k1, lambda_q2, lambda_k2, subln_g, w_o, w_router, router_bias, w_gate, w_up, w_down):
    d = D_MODEL
    bs = x_sample.shape[0]
    bp, sp = x_prompt.shape[:2]
    assert w_ada.shape[0] == 2 and w_in_ab.shape[0] == 1 and w_qkv.shape[0] == 1

    cond_rows = 16
    cond = jnp.zeros((cond_rows, d), F32).at[:bs].set(c).at[bs].set(c_ctx)
    mod = _ada_params(cond, w_ada, b_ada)
    mod_s = mod[:, :bs].reshape(2, bs, 1, 6 * d)
    mod_p = mod[:, bs:bs + 1].reshape(2, 1, 1, 6 * d)

    w = dict(
        norm1_g=norm1_g, norm2_g=norm2_g, final_g=final_norm_g.reshape(1, d),
        w_in=w_in_ab[0].astype(BF16), cs_tab=_channel_dft_table(),
        w_fourier=w_fourier[0].astype(BF16), w_pool=w_pool[0].astype(BF16), pool_scale=pool_scale[0:1],
        w_out_ab=w_out_ab[0].astype(BF16), w_qkv=w_qkv[0].astype(BF16), w_o=w_o[0].astype(BF16),
        lam=jnp.concatenate([lambda_q1, lambda_k1, lambda_q2, lambda_k2], axis=0),
        subln_g=subln_g[0:1], wr_t=w_router.T, rb=router_bias.reshape(N_EXPERTS, 1),
        wgu=jnp.concatenate([w_gate, w_up], axis=-1).astype(BF16), wd=w_down.astype(BF16))

    p = cache_k.shape[2]
    kc = cache_k[:, 0].reshape(bs * p, d).astype(BF16)
    vc = cache_v[:, 0].reshape(bs * p, d).astype(BF16)

    y_prompt, (kn, vn) = _stream(x_prompt, mod_p, None, w, rope=False)
    y_sample, _ = _stream(x_sample, mod_s, (kc, vc), w, rope=True)
    new_cache_k = kn.reshape(bp, 1, sp, N_HEADS, 2 * HEAD_DIM)
    new_cache_v = vn.reshape(bp, 1, sp, N_HEADS, VALUE_DIM)
    return (y_prompt, y_sample, new_cache_k, new_cache_v)
```
